```python
import jax, jax.numpy as jnp
from jax import lax
import numpy as np

D_MODEL = 1024
BATCH = 8
SEQ = 8192
DEPTH = 1
DEC_BATCH = 32
DEC_SEQ = 2048
PAST_LEN = 128

MIX_WIDTH = D_MODEL
W_A = MIX_WIDTH // 2
W_B = MIX_WIDTH - W_A
HEAD_DIM = 64
N_HEADS_A = W_A // HEAD_DIM
N_HEADS_B = W_B // HEAD_DIM
CONV_A = 3
CONV_B = 31
RMS_EPS = 1e-6
LN_EPS = 1e-5
IN_COLS = 4 * W_A + 3 * W_B
SPLITS = (W_A, 2 * W_A, 3 * W_A, 4 * W_A, 4 * W_A + W_B, 4 * W_A + 2 * W_B)

kernel_name = "hybrid_gated_conv_conformer_encoder"


def rmsnorm(x, g):
    xf = x.astype(jnp.float32)
    y = xf * lax.rsqrt(jnp.mean(xf * xf, axis=-1, keepdims=True) + RMS_EPS)
    return (y * g.astype(jnp.float32)).astype(x.dtype)


def dwconv(x, w):
    k = w.shape[0]
    pad = k // 2
    return lax.conv_general_dilated(
        x, w[:, None, :].astype(x.dtype), window_strides=(1,), padding=[(pad, pad)],
        dimension_numbers=("NWC", "WIO", "NWC"), feature_group_count=x.shape[-1])


def head_layernorm(x, g, b):
    bsz, L, _ = x.shape
    xf = x.astype(jnp.float32).reshape(bsz, L, N_HEADS_B, HEAD_DIM)
    mu = jnp.mean(xf, axis=-1, keepdims=True)
    var = jnp.mean(jnp.square(xf - mu), axis=-1, keepdims=True)
    y = ((xf - mu) * lax.rsqrt(var + LN_EPS)).reshape(bsz, L, W_B)
    return (y * g.astype(jnp.float32) + b.astype(jnp.float32)).astype(x.dtype)


def mixer_layer(x, c, norm_g, w_ada, b_ada, w_in, conv_a_w, conv_b_w, conv_b_b, ln_g, ln_b, w_out):
    mod = jax.nn.silu(c) @ w_ada + b_ada
    shift, scale, gate = jnp.split(mod, 3, axis=-1)
    h = rmsnorm(x, norm_g) * (1.0 + scale[:, None, :]) + shift[:, None, :]
    p = h @ w_in
    a_in, a_b, a_c, a_z, b_v, b_g, b_z = jnp.split(p, SPLITS, axis=-1)
    y_a = a_b * dwconv(a_c * a_in, conv_a_w) * jax.nn.silu(a_z)
    u = b_v * jax.nn.sigmoid(b_g)
    u = dwconv(u, conv_b_w) + conv_b_b
    y_b = jax.nn.silu(head_layernorm(u, ln_g, ln_b)) * jax.nn.silu(b_z)
    y = jnp.concatenate([y_a, y_b], axis=-1) @ w_out
    return x + gate[:, None, :] * y


def run_trunk(x, c, norm_g, w_ada, b_ada, w_in, conv_a_w, conv_b_w, conv_b_b, ln_g, ln_b, w_out, final_g):
    for l in range(DEPTH):
        x = mixer_layer(x, c, norm_g[l], w_ada[l], b_ada[l], w_in[l], conv_a_w[l],
                        conv_b_w[l], conv_b_b[l], ln_g[l], ln_b[l], w_out[l])
    return rmsnorm(x, final_g)


def setup_inputs(seed: int = 0) -> dict:
    key = jax.random.key(seed)
    ks = jax.random.split(key, 16)
    f32 = jnp.float32
    D = D_MODEL
    x_prompt = jax.random.normal(ks[0], (BATCH, SEQ, D), f32)
    x_sample = jax.random.normal(ks[1], (DEC_BATCH, DEC_SEQ, D), f32)
    c_prompt = jax.random.normal(ks[2], (BATCH, D), f32)
    c_sample = jax.random.normal(ks[3], (DEC_BATCH, D), f32)
    norm_g = 1.0 + 0.02 * jax.random.normal(ks[4], (DEPTH, D), f32)
    w_ada = jax.random.normal(ks[5], (DEPTH, D, 3 * D), f32) * (D ** -0.5)
    b_ada = 0.02 * jax.random.normal(ks[6], (DEPTH, 3 * D), f32)
    w_in = jax.random.normal(ks[7], (DEPTH, D, IN_COLS), f32) * (D ** -0.5)
    conv_a_w = jax.random.normal(ks[8], (DEPTH, CONV_A, W_A), f32) * (CONV_A ** -0.5)
    conv_b_w = jax.random.normal(ks[9], (DEPTH, CONV_B, W_B), f32) * (CONV_B ** -0.5)
    conv_b_b = 0.02 * jax.random.normal(ks[10], (DEPTH, W_B), f32)
    ln_g = 1.0 + 0.02 * jax.random.normal(ks[11], (DEPTH, W_B), f32)
    ln_b = 0.02 * jax.random.normal(ks[12], (DEPTH, W_B), f32)
    w_out = jax.random.normal(ks[13], (DEPTH, MIX_WIDTH, D), f32) * (MIX_WIDTH ** -0.5)
    final_g = 1.0 + 0.02 * jax.random.normal(ks[14], (D,), f32)
    return {"x_prompt": x_prompt, "x_sample": x_sample, "c_prompt": c_prompt, "c_sample": c_sample,
            "norm_g": norm_g, "w_ada": w_ada, "b_ada": b_ada, "w_in": w_in,
            "conv_a_w": conv_a_w, "conv_b_w": conv_b_w, "conv_b_b": conv_b_b,
            "ln_g": ln_g, "ln_b": ln_b, "w_out": w_out, "final_g": final_g}


def reference(x_prompt, x_sample, c_prompt, c_sample, norm_g, w_ada, b_ada, w_in,
              conv_a_w, conv_b_w, conv_b_b, ln_g, ln_b, w_out, final_g):
    y_prompt = run_trunk(x_prompt, c_prompt, norm_g, w_ada, b_ada, w_in, conv_a_w,
                         conv_b_w, conv_b_b, ln_g, ln_b, w_out, final_g)
    y_sample = run_trunk(x_sample, c_sample, norm_g, w_ada, b_ada, w_in, conv_a_w,
                         conv_b_w, conv_b_b, ln_g, ln_b, w_out, final_g)
    return (y_prompt, y_sample)
```

```python
import functools

import jax
import jax.numpy as jnp
from jax import lax
from jax.experimental import pallas as pl
from jax.experimental.pallas import tpu as pltpu

D_MODEL = 1024
W_A = 512
W_B = 512
HEAD_DIM = 64
CONV_A = 3
CONV_B = 31
RMS_EPS = 1e-6
LN_EPS = 1e-5
IN_COLS = 4 * W_A + 3 * W_B

C_AIN, C_AB, C_AC, C_AZ, C_BV, C_BG, C_BZ = (0, 512, 1024, 1536, 2048, 2560, 3072)

LANES = 128
SUBLANES = 8
MXU_DIM = 256
HALO = 16
SEQ_TILE = 512
CONV_ROWS = 64
VMEM_LIMIT = 48 * 1024 * 1024

F32 = jnp.float32
BF16 = jnp.bfloat16


def _sigmoid(x):
    return 1.0 / (1.0 + jnp.exp(-x))


def _silu(x):
    return x * _sigmoid(x)


def _ada_kernel(c_ref, w_ref, b_ref, ng_ref, out_ref):
    j = pl.program_id(0)
    c = c_ref[...]
    s = _silu(c)
    mod = jnp.dot(s, w_ref[...], precision=lax.Precision.HIGHEST,
                  preferred_element_type=F32) + b_ref[...]
    out_ref[0] = jnp.where(j == 1, ng_ref[...] * (1.0 + mod), mod)


def _ada_call(c_all, w_ada, b_ada, norm_g):
    n = c_all.shape[0]
    return pl.pallas_call(
        _ada_kernel,
        grid=(3,),
        in_specs=[
            pl.BlockSpec((n, D_MODEL), lambda j: (0, 0)),
            pl.BlockSpec((D_MODEL, D_MODEL), lambda j: (0, j)),
            pl.BlockSpec((1, D_MODEL), lambda j: (0, j)),
            pl.BlockSpec((1, D_MODEL), lambda j: (0, 0)),
        ],
        out_specs=pl.BlockSpec((1, n, D_MODEL), lambda j: (j, 0, 0)),
        out_shape=jax.ShapeDtypeStruct((3, n, D_MODEL), F32),
        compiler_params=pltpu.CompilerParams(
            dimension_semantics=("arbitrary",), vmem_limit_bytes=VMEM_LIMIT),
        name="adaln_mod",
    )(c_all, w_ada, b_ada.reshape(1, -1), norm_g.reshape(1, -1))


def _mixer_kernel(xm_ref, xp_ref, xn_ref, shift_ref, gs_ref, gate_ref, w_in_ref,
                  caw_ref, cbw_ref, cbb_ref, lng_ref, lnb_ref, m_ref, w_out_ref, fg_ref,
                  out_ref, hbuf, cabuf, gabuf, ubuf, gbbuf, ybuf, u2buf):
    T = SEQ_TILE
    i = pl.program_id(1)
    first = i == 0
    last = i == pl.num_programs(1) - 1

    shift = shift_ref[0]
    gs = gs_ref[0]
    gate = gate_ref[0]

    def norm_mod(x):
        ms = jnp.mean(x * x, axis=-1, keepdims=True)
        return ((x * lax.rsqrt(ms + RMS_EPS)) * gs + shift).astype(BF16)

    hbuf[0:HALO, :] = norm_mod(xp_ref[0])
    hbuf[HALO + T:HALO + T + HALO, :] = norm_mod(xn_ref[0])

    def p1(j, carry):
        r0 = pl.multiple_of(j * 64, 64)
        hbuf[pl.ds(HALO + r0, 64), :] = norm_mod(xm_ref[0, pl.ds(r0, 64), :])
        return carry
    lax.fori_loop(0, T // 64, p1, 0)

    def proj(h, col):
        return jnp.dot(h, w_in_ref[:, col:col + W_A], preferred_element_type=F32)

    def to_slabs(buf, r0, val):
        for c in range(val.shape[1] // LANES):
            buf[c, r0:r0 + val.shape[0], :] = val[:, c * LANES:(c + 1) * LANES]

    half = (T + 2 * HALO) // 2
    for r0 in (0, half):
        h = hbuf[r0:r0 + half, :]
        to_slabs(cabuf, r0, proj(h, C_AIN) * proj(h, C_AC))
        to_slabs(ubuf, r0, proj(h, C_BV) * _sigmoid(proj(h, C_BG)))
    for r0 in (0, T // 2):
        h = hbuf[HALO + r0:HALO + r0 + T // 2, :]
        gabuf[r0:r0 + T // 2, :] = proj(h, C_AB) * _silu(proj(h, C_AZ))
        gbbuf[r0:r0 + T // 2, :] = _silu(proj(h, C_BZ))

    nslab = W_A // LANES

    @pl.when(first)
    def _():
        cabuf[:, 0:HALO, :] = jnp.zeros((nslab, HALO, LANES), F32)
        ubuf[:, 0:HALO, :] = jnp.zeros((nslab, HALO, LANES), F32)

    @pl.when(last)
    def _():
        cabuf[:, HALO + T:HALO + T + HALO, :] = jnp.zeros((nslab, HALO, LANES), F32)
        ubuf[:, HALO + T:HALO + T + HALO, :] = jnp.zeros((nslab, HALO, LANES), F32)

    R = CONV_ROWS

    def rows_at(buf, c, start, n):
        return buf[pl.ds(c, 1, stride=2), pl.ds(start, n), :][0]

    def conv(j, carry):
        r0 = pl.multiple_of(j * R, R)
        for c in range(nslab):
            cs = pl.ds(c * LANES, LANES)
            acc = rows_at(cabuf, c, r0 + HALO - 1, R) * caw_ref[0:1, cs]
            acc = acc + rows_at(cabuf, c, r0 + HALO, R) * caw_ref[1:2, cs]
            acc = acc + rows_at(cabuf, c, r0 + HALO + 1, R) * caw_ref[2:3, cs]
            ybuf[pl.ds(r0, R), cs] = (acc * gabuf[pl.ds(r0, R), cs]).astype(BF16)
            acc = jnp.zeros((R, LANES), F32) + cbb_ref[0:1, cs]
            for b in range(SUBLANES):
                win = rows_at(ubuf, c, r0 + b + 1, R + 3 * SUBLANES)
                for a in range(4):
                    k = SUBLANES * a + b
                    if k < CONV_B:
                        acc = acc + win[SUBLANES * a:SUBLANES * a + R] * cbw_ref[k:k + 1, cs]
            u2buf[pl.ds(r0, R), cs] = acc
        return carry
    lax.fori_loop(0, T // R, conv, 0)

    mmat = m_ref[...]
    for hcol in range(W_B // MXU_DIM):
        cs = slice(hcol * MXU_DIM, (hcol + 1) * MXU_DIM)
        u2 = u2buf[:, cs]
        mu = jnp.dot(u2.astype(BF16), mmat, preferred_element_type=F32)
        d = u2 - mu
        var = jnp.dot((d * d).astype(BF16), mmat, preferred_element_type=F32)
        yn = d * lax.rsqrt(var + LN_EPS) * lng_ref[0:1, cs] + lnb_ref[0:1, cs]
        ybuf[:, W_A + hcol * MXU_DIM:W_A + (hcol + 1) * MXU_DIM] = (
            _silu(yn) * gbbuf[:, cs]).astype(BF16)

    fg = fg_ref[...]
    for r0 in (0, T // 2):
        o = jnp.dot(ybuf[r0:r0 + T // 2, :], w_out_ref[...], preferred_element_type=F32)
        z = xm_ref[0, r0:r0 + T // 2, :] + gate * o
        ms = jnp.mean(z * z, axis=-1, keepdims=True)
        out_ref[0, r0:r0 + T // 2, :] = z * lax.rsqrt(ms + RMS_EPS) * fg


def _mixer_call(x, shift, gs, gate, w_in, caw, cbw, cbb, lng, lnb, mmat, w_out, fg):
    B, L, D = x.shape
    T = SEQ_TILE
    assert L % T == 0 and T % HALO == 0
    nt = L // T
    hb = T // HALO
    nhb = L // HALO

    def const(shape):
        return pl.BlockSpec(shape, lambda b, i: (0,) * len(shape),
                            pipeline_mode=pl.Buffered(1))

    per_b = pl.BlockSpec((1, 1, D), lambda b, i: (b, 0, 0))
    return pl.pallas_call(
        _mixer_kernel,
        grid=(B, nt),
        in_specs=[
            pl.BlockSpec((1, T, D), lambda b, i: (b, i, 0)),
            pl.BlockSpec((1, HALO, D), lambda b, i: (b, jnp.maximum(i * hb - 1, 0), 0)),
            pl.BlockSpec((1, HALO, D), lambda b, i: (b, jnp.minimum((i + 1) * hb, nhb - 1), 0)),
            per_b, per_b, per_b,
            const((D, IN_COLS)),
            const((CONV_A, W_A)),
            const((CONV_B, W_B)),
            const((1, W_B)),
            const((1, W_B)),
            const((1, W_B)),
            const((MXU_DIM, MXU_DIM)),
            const((D, D)),
            const((1, D)),
        ],
        out_specs=pl.BlockSpec((1, T, D), lambda b, i: (b, i, 0)),
        out_shape=jax.ShapeDtypeStruct((B, L, D), F32),
        scratch_shapes=[
            pltpu.VMEM((T + 2 * HALO, D), BF16),
            pltpu.VMEM((W_A // LANES, T + 2 * HALO, LANES), F32),
            pltpu.VMEM((T, W_A), F32),
            pltpu.VMEM((W_B // LANES, T + 2 * HALO, LANES), F32),
            pltpu.VMEM((T, W_B), F32),
            pltpu.VMEM((T, D), BF16),
            pltpu.VMEM((T, W_B), F32),
        ],
        compiler_params=pltpu.CompilerParams(
            dimension_semantics=("arbitrary", "arbitrary"), vmem_limit_bytes=VMEM_LIMIT),
        name="mixer_block",
    )(x, x, x, shift, gs, gate, w_in, caw, cbw, cbb, lng, lnb, mmat, w_out, fg)


def kernel(x_prompt, x_sample, c_prompt, c_sample, norm_g, w_ada, b_ada, w_in, conv_a_w,
           conv_b_w, conv_b_b, ln_g, ln_b, w_out, final_g):
    nb_p = c_prompt.shape[0]
    c_all = jnp.concatenate([c_prompt, c_sample], axis=0)
    mod = _ada_call(c_all, w_ada[0], b_ada[0], norm_g[0])
    mod = mod[:, :, None, :]

    w_in_b = w_in[0].astype(BF16)
    w_out_b = w_out[0].astype(BF16)
    head = jnp.arange(MXU_DIM) // HEAD_DIM
    mmat = jnp.where(head[:, None] == head[None, :], 1.0 / HEAD_DIM, 0.0).astype(BF16)

    def run(x, lo, hi):
        return _mixer_call(x, mod[0, lo:hi], mod[1, lo:hi], mod[2, lo:hi], w_in_b,
                           conv_a_w[0], conv_b_w[0], conv_b_b[0].reshape(1, -1),
                           ln_g[0].reshape(1, -1), ln_b[0].reshape(1, -1), mmat, w_out_b,
                           final_g.reshape(1, -1))

    y_prompt = run(x_prompt, 0, nb_p)
    y_sample = run(x_sample, nb_p, c_all.shape[0])
    return (y_prompt, y_sample)
```

```python
import functools

import jax
import jax.numpy as jnp
from jax import lax
from jax.experimental import pallas as pl
from jax.experimental.pallas import tpu as pltpu

D_MODEL = 1024
W_A = 512
W_B = 512
HEAD_DIM = 64
CONV_A = 3
CONV_B = 31
RMS_EPS = 1e-6
LN_EPS = 1e-5
IN_COLS = 4 * W_A + 3 * W_B

C_AIN, C_AB, C_AC, C_AZ, C_BV, C_BG, C_BZ = (0, 512, 1024, 1536, 2048, 2560, 3072)
BLK_U, BLK_CA, BLK_GA, BLK_GB = 0, 4, 8, 12

LANES = 128
SUBLANES = 8
MXU_DIM = 256
HALO = 16
SEQ_TILE = 512
CHUNK = 256
CONV_ROWS = 64
NORM_ROWS = 32
SLOT = SEQ_TILE + 2 * HALO
NSLAB = W_A // LANES
VMEM_LIMIT = 48 * 1024 * 1024

F32 = jnp.float32
BF16 = jnp.bfloat16


def _sigmoid(x):
    return 1.0 / (1.0 + jnp.exp(-x))


def _silu(x):
    return x * _sigmoid(x)


def _ada_kernel(c_ref, w_ref, b_ref, ng_ref, out_ref):
    j = pl.program_id(0)
    c = c_ref[...]
    s = _silu(c)
    mod = jnp.dot(s, w_ref[...], precision=lax.Precision.HIGHEST,
                  preferred_element_type=F32) + b_ref[...]
    out_ref[0] = jnp.where(j == 1, ng_ref[...] * (1.0 + mod), mod)


def _ada_call(c_all, w_ada, b_ada, norm_g):
    n = c_all.shape[0]
    return pl.pallas_call(
        _ada_kernel,
        grid=(3,),
        in_specs=[
            pl.BlockSpec((n, D_MODEL), lambda j: (0, 0)),
            pl.BlockSpec((D_MODEL, D_MODEL), lambda j: (0, j)),
            pl.BlockSpec((1, D_MODEL), lambda j: (0, j)),
            pl.BlockSpec((1, D_MODEL), lambda j: (0, 0)),
        ],
        out_specs=pl.BlockSpec((1, n, D_MODEL), lambda j: (j, 0, 0)),
        out_shape=jax.ShapeDtypeStruct((3, n, D_MODEL), F32),
        compiler_params=pltpu.CompilerParams(
            dimension_semantics=("arbitrary",), vmem_limit_bytes=VMEM_LIMIT),
        name="adaln_mod",
    )(c_all, w_ada, b_ada.reshape(1, -1), norm_g.reshape(1, -1))


def _rows_at(buf, slab, start, n):
    return buf[pl.ds(slab, 1, stride=2), pl.ds(start, n), :][0]


def _mixer_kernel(x_ref, shift_ref, gs_ref, gate_ref, w_in_ref,
                  caw_ref, cbw_ref, cbb_ref, lng_ref, lnb_ref, m_ref, w_out_ref, fg_ref,
                  out_ref, hbuf, xbuf, cabuf, gabuf, ubuf, gbbuf, ybuf, u2buf, dbuf, obuf,
                  *, nt):
    T, C, R = SEQ_TILE, CHUNK, CONV_ROWS
    NB = MXU_DIM
    nchunk = T // C
    s = pl.program_id(0)
    first_a = (s % nt) == 0
    last_a = (s % nt) == nt - 1

    @pl.when(s == 0)
    def _():
        xbuf[...] = jnp.zeros(xbuf.shape, F32)
        cabuf[...] = jnp.zeros(cabuf.shape, F32)
        gabuf[...] = jnp.zeros(gabuf.shape, F32)
        ubuf[...] = jnp.zeros(ubuf.shape, F32)
        gbbuf[...] = jnp.zeros(gbbuf.shape, F32)

    q = s % 2
    o = 1 - q
    body_a = pl.multiple_of(q * SLOT + HALO, HALO)
    body_b = pl.multiple_of(o * SLOT + HALO, HALO)
    pt_o = pl.multiple_of(o * SLOT, HALO)
    nh_o = pl.multiple_of(o * SLOT + HALO + T, HALO)

    def chunk(c, carry):
        r0 = pl.multiple_of(c * C, C)
        shift = shift_ref[0]
        gs = gs_ref[0]
        gate = gate_ref[0]
        fg = fg_ref[...]
        mmat = m_ref[...]

        def proj(blk):
            return jnp.dot(hbuf[...], w_in_ref[:, blk * NB:(blk + 1) * NB],
                           preferred_element_type=F32)

        def norm_item():
            for i in range(C // NORM_ROWS):
                rr = pl.multiple_of(r0 + i * NORM_ROWS, NORM_ROWS)
                x = x_ref[pl.ds(rr, NORM_ROWS), :]
                xbuf[q, pl.ds(rr, NORM_ROWS), :] = x
                ms = jnp.mean(x * x, axis=-1, keepdims=True)
                hbuf[i * NORM_ROWS:(i + 1) * NORM_ROWS, :] = (
                    (x * lax.rsqrt(ms + RMS_EPS)) * gs + shift).astype(BF16)

        def conv_input_item(slab, buf, blk0, combine):
            res = proj(blk0 + slab)
            val = combine(res[:, :LANES], res[:, LANES:])
            buf[slab, pl.ds(body_a + r0, C), :] = val
            head = jnp.where(first_a, 0.0, val[0:HALO])
            buf[slab, pl.ds(nh_o, HALO), :] = jnp.where(
                c == 0, head, buf[slab, pl.ds(nh_o, HALO), :])
            tail = jnp.where(last_a, 0.0, val[C - HALO:C])
            buf[slab, pl.ds(pt_o, HALO), :] = jnp.where(
                c == nchunk - 1, tail, buf[slab, pl.ds(pt_o, HALO), :])

        def ga_item(slab):
            res = proj(BLK_GA + slab)
            gabuf[q, pl.ds(r0, C), slab * LANES:(slab + 1) * LANES] = (
                res[:, :LANES] * _silu(res[:, LANES:]))

        def gb_item(blk):
            res = proj(BLK_GB + blk)
            gbbuf[q, pl.ds(r0, C), blk * NB:(blk + 1) * NB] = _silu(res)

        def conv_unit(j, cl):
            ru = pl.multiple_of(r0 + j * R, R)
            cs = pl.ds(cl * LANES, LANES)
            ls = slice(cl * LANES, (cl + 1) * LANES)
            rows = slice(j * R, (j + 1) * R)
            acc = _rows_at(cabuf, cl, body_b + ru - 1, R) * caw_ref[0:1, cs]
            acc = acc + _rows_at(cabuf, cl, body_b + ru, R) * caw_ref[1:2, cs]
            acc = acc + _rows_at(cabuf, cl, body_b + ru + 1, R) * caw_ref[2:3, cs]
            ybuf[rows, ls] = (acc * gabuf[o, pl.ds(ru, R), ls]).astype(BF16)
            acc = jnp.zeros((R, LANES), F32) + cbb_ref[0:1, cs]
            for b in range(SUBLANES):
                win = _rows_at(ubuf, cl, body_b + ru + b - (CONV_B // 2), R + 3 * SUBLANES)
                for a in range(4):
                    k = SUBLANES * a + b
                    if k < CONV_B:
                        acc = acc + win[SUBLANES * a:SUBLANES * a + R] * cbw_ref[k:k + 1, cs]
            u2buf[rows, ls] = acc

        def headnorm_item(hcol):
            cs = slice(hcol * NB, (hcol + 1) * NB)
            mu = jnp.dot(u2buf[:, cs].astype(BF16), mmat, preferred_element_type=F32)
            dbuf[:, cs] = u2buf[:, cs] - mu
            d = dbuf[:, cs]
            var = jnp.dot((d * d).astype(BF16), mmat, preferred_element_type=F32)
            yn = dbuf[:, cs] * lax.rsqrt(var + LN_EPS) * lng_ref[0:1, cs] + lnb_ref[0:1, cs]
            ybuf[:, W_A + hcol * NB:W_A + (hcol + 1) * NB] = (
                _silu(yn) * gbbuf[o, pl.ds(r0, C), cs]).astype(BF16)

        def out_item():
            obuf[...] = jnp.dot(ybuf[...], w_out_ref[...], preferred_element_type=F32)
            for i in range(C // NORM_ROWS):
                rr = pl.multiple_of(r0 + i * NORM_ROWS, NORM_ROWS)
                z = (xbuf[o, pl.ds(rr, NORM_ROWS), :]
                     + gate * obuf[i * NORM_ROWS:(i + 1) * NORM_ROWS, :])
                ms = jnp.mean(z * z, axis=-1, keepdims=True)
                out_ref[pl.ds(rr, NORM_ROWS), :] = z * lax.rsqrt(ms + RMS_EPS) * fg

        def glu(v, g):
            return v * _sigmoid(g)

        def prod(a, b):
            return a * b

        norm_item()
        for slab in range(NSLAB):
            conv_input_item(slab, ubuf, BLK_U, glu)
        for slab in range(NSLAB):
            conv_input_item(slab, cabuf, BLK_CA, prod)
        for slab in range(NSLAB):
            ga_item(slab)
        for blk in range(W_B // NB):
            gb_item(blk)
        for hcol in range(W_B // NB):
            for j in range(C // R):
                for cl in range(hcol * (NB // LANES), (hcol + 1) * (NB // LANES)):
                    conv_unit(j, cl)
        for hcol in range(W_B // NB):
            headnorm_item(hcol)
        out_item()
        return carry

    lax.fori_loop(0, nchunk, chunk, 0)


def _mixer_call(x, shift, gs, gate, w_in, caw, cbw, cbb, lng, lnb, mmat, w_out, fg):
    B, L, D = x.shape
    T = SEQ_TILE
    assert L % T == 0 and T % CHUNK == 0 and CHUNK % CONV_ROWS == 0
    nt = L // T
    ntiles = B * nt
    x2 = x.reshape(B * L, D)

    def const(shape):
        return pl.BlockSpec(shape, lambda s: (0,) * len(shape),
                            pipeline_mode=pl.Buffered(1))

    def tile_a(s):
        return jnp.minimum(s, ntiles - 1)

    def tile_b(s):
        return jnp.maximum(s - 1, 0)

    out = pl.pallas_call(
        functools.partial(_mixer_kernel, nt=nt),
        grid=(ntiles + 1,),
        in_specs=[
            pl.BlockSpec((T, D), lambda s: (tile_a(s), 0)),
            pl.BlockSpec((1, 1, D), lambda s: (tile_a(s) // nt, 0, 0)),
            pl.BlockSpec((1, 1, D), lambda s: (tile_a(s) // nt, 0, 0)),
            pl.BlockSpec((1, 1, D), lambda s: (tile_b(s) // nt, 0, 0)),
            const((D, IN_COLS)),
            const((CONV_A, W_A)),
            const((CONV_B, W_B)),
            const((1, W_B)),
            const((1, W_B)),
            const((1, W_B)),
            const((MXU_DIM, MXU_DIM)),
            const((D, D)),
            const((1, D)),
        ],
        out_specs=pl.BlockSpec((T, D), lambda s: (tile_b(s), 0)),
        out_shape=jax.ShapeDtypeStruct((B * L, D), F32),
        scratch_shapes=[
            pltpu.VMEM((CHUNK, D), BF16),
            pltpu.VMEM((2, T, D), F32),
            pltpu.VMEM((NSLAB, 2 * SLOT, LANES), F32),
            pltpu.VMEM((2, T, W_A), F32),
            pltpu.VMEM((NSLAB, 2 * SLOT, LANES), F32),
            pltpu.VMEM((2, T, W_B), F32),
            pltpu.VMEM((CHUNK, D), BF16),
            pltpu.VMEM((CHUNK, W_B), F32),
            pltpu.VMEM((CHUNK, W_B), F32),
            pltpu.VMEM((CHUNK, D), F32),
        ],
        compiler_params=pltpu.CompilerParams(
            dimension_semantics=("arbitrary",), vmem_limit_bytes=VMEM_LIMIT),
        name="mixer_block",
    )(x2, shift, gs, gate, w_in, caw, cbw, cbb, lng, lnb, mmat, w_out, fg)
    return out.reshape(B, L, D)


def _permute_in_cols(w):
    def slab(col, j):
        return w[:, col + j * LANES:col + (j + 1) * LANES]
    parts = []
    for c1, c2 in ((C_BV, C_BG), (C_AIN, C_AC), (C_AB, C_AZ)):
        for j in range(NSLAB):
            parts += [slab(c1, j), slab(c2, j)]
    parts.append(w[:, C_BZ:C_BZ + W_B])
    return jnp.concatenate(parts, axis=1)


def kernel(x_prompt, x_sample, c_prompt, c_sample, norm_g, w_ada, b_ada, w_in, conv_a_w,
           conv_b_w, conv_b_b, ln_g, ln_b, w_out, final_g):
    nb_p = c_prompt.shape[0]
    c_all = jnp.concatenate([c_prompt, c_sample], axis=0)
    mod = _ada_call(c_all, w_ada[0], b_ada[0], norm_g[0])
    mod = mod[:, :, None, :]

    w_in_b = _permute_in_cols(w_in[0]).astype(BF16)
    w_out_b = w_out[0].astype(BF16)
    head = jnp.arange(MXU_DIM) // HEAD_DIM
    mmat = jnp.where(head[:, None] == head[None, :], 1.0 / HEAD_DIM, 0.0).astype(BF16)

    def run(x, lo, hi):
        return _mixer_call(x, mod[0, lo:hi], mod[1, lo:hi], mod[2, lo:hi], w_in_b,
                           conv_a_w[0], conv_b_w[0], conv_b_b[0].reshape(1, -1),
                           ln_g[0].reshape(1, -1), ln_b[0].reshape(1, -1), mmat, w_out_b,
                           final_g.reshape(1, -1))

    y_prompt = run(x_prompt, 0, nb_p)
    y_sample = run(x_sample, nb_p, c_all.shape[0])
    return (y_prompt, y_sample)
```

```python
import functools

import jax
import jax.numpy as jnp
from jax import lax
from jax.experimental import pallas as pl
from jax.experimental.pallas import tpu as pltpu

D_MODEL = 1024
W_A = 512
W_B = 512
HEAD_DIM = 64
CONV_A = 3
CONV_B = 31
RMS_EPS = 1e-6
LN_EPS = 1e-5
IN_COLS = 4 * W_A + 3 * W_B

C_AIN, C_AB, C_AC, C_AZ, C_BV, C_BG, C_BZ = (0, 512, 1024, 1536, 2048, 2560, 3072)
BLK_U, BLK_CA, BLK_GA, BLK_GB = 0, 4, 8, 12

LANES = 128
SUBLANES = 8
MXU_DIM = 256
HALO = 16
SEQ_TILE = 1024
CHUNK = 256
CONV_ROWS = 64
NORM_ROWS = 32
SLOT = SEQ_TILE + 2 * HALO
NSLAB = W_A // LANES
VMEM_LIMIT = 58 * 1024 * 1024

F32 = jnp.float32
BF16 = jnp.bfloat16


def _sigmoid(x):
    return 1.0 / (1.0 + jnp.exp(-x))


def _silu(x):
    return x * _sigmoid(x)


def _ada_kernel(c_ref, w_ref, b_ref, ng_ref, out_ref):
    j = pl.program_id(0)
    c = c_ref[...]
    s = _silu(c)
    mod = jnp.dot(s, w_ref[...], precision=lax.Precision.HIGHEST,
                  preferred_element_type=F32) + b_ref[...]
    out_ref[0] = jnp.where(j == 1, ng_ref[...] * (1.0 + mod), mod)


def _ada_call(c_all, w_ada, b_ada, norm_g):
    n = c_all.shape[0]
    return pl.pallas_call(
        _ada_kernel,
        grid=(3,),
        in_specs=[
            pl.BlockSpec((n, D_MODEL), lambda j: (0, 0)),
            pl.BlockSpec((D_MODEL, D_MODEL), lambda j: (0, j)),
            pl.BlockSpec((1, D_MODEL), lambda j: (0, j)),
            pl.BlockSpec((1, D_MODEL), lambda j: (0, 0)),
        ],
        out_specs=pl.BlockSpec((1, n, D_MODEL), lambda j: (j, 0, 0)),
        out_shape=jax.ShapeDtypeStruct((3, n, D_MODEL), F32),
        compiler_params=pltpu.CompilerParams(
            dimension_semantics=("arbitrary",), vmem_limit_bytes=VMEM_LIMIT),
        name="adaln_mod",
    )(c_all, w_ada, b_ada.reshape(1, -1), norm_g.reshape(1, -1))


def _rows_at(buf, slab, start, n):
    return buf[pl.ds(slab, 1, stride=2), pl.ds(start, n), :][0]


def _mixer_kernel(x_ref, shift_ref, gs_ref, gate_ref, w_in_ref,
                  caw_ref, cbw_ref, cbb_ref, lng_ref, lnb_ref, m_ref, w_out_ref, fg_ref,
                  out_ref, hbuf, xbuf, cabuf, gabuf, ubuf, gbbuf, ybuf, u2buf, dbuf, obuf,
                  *, nt):
    T, C, R = SEQ_TILE, CHUNK, CONV_ROWS
    NB = MXU_DIM
    nchunk = T // C
    s = pl.program_id(0)
    first_a = (s % nt) == 0
    last_a = (s % nt) == nt - 1

    @pl.when(s == 0)
    def _():
        xbuf[...] = jnp.zeros(xbuf.shape, F32)
        cabuf[...] = jnp.zeros(cabuf.shape, F32)
        gabuf[...] = jnp.zeros(gabuf.shape, F32)
        ubuf[...] = jnp.zeros(ubuf.shape, F32)
        gbbuf[...] = jnp.zeros(gbbuf.shape, F32)

    q = s % 2
    o = 1 - q
    body_a = pl.multiple_of(q * SLOT + HALO, HALO)
    body_b = pl.multiple_of(o * SLOT + HALO, HALO)
    pt_o = pl.multiple_of(o * SLOT, HALO)
    nh_o = pl.multiple_of(o * SLOT + HALO + T, HALO)

    def chunk(c, carry):
        r0 = pl.multiple_of(c * C, C)
        shift = shift_ref[0]
        gs = gs_ref[0]
        gate = gate_ref[0]
        fg = fg_ref[...]
        mmat = m_ref[...]

        def proj(blk):
            return jnp.dot(hbuf[...], w_in_ref[:, blk * NB:(blk + 1) * NB],
                           preferred_element_type=F32)

        def norm_item():
            for i in range(C // NORM_ROWS):
                rr = pl.multiple_of(r0 + i * NORM_ROWS, NORM_ROWS)
                x = x_ref[pl.ds(rr, NORM_ROWS), :]
                xbuf[q, pl.ds(rr, NORM_ROWS), :] = x
                ms = jnp.mean(x * x, axis=-1, keepdims=True)
                hbuf[i * NORM_ROWS:(i + 1) * NORM_ROWS, :] = (
                    (x * lax.rsqrt(ms + RMS_EPS)) * gs + shift).astype(BF16)

        def conv_input_item(slab, buf, blk0, combine):
            res = proj(blk0 + slab)
            val = combine(res[:, :LANES], res[:, LANES:])
            buf[slab, pl.ds(body_a + r0, C), :] = val
            head = jnp.where(first_a, 0.0, val[0:HALO])
            buf[slab, pl.ds(nh_o, HALO), :] = jnp.where(
                c == 0, head, buf[slab, pl.ds(nh_o, HALO), :])
            tail = jnp.where(last_a, 0.0, val[C - HALO:C])
            buf[slab, pl.ds(pt_o, HALO), :] = jnp.where(
                c == nchunk - 1, tail, buf[slab, pl.ds(pt_o, HALO), :])

        def ga_item(slab):
            res = proj(BLK_GA + slab)
            gabuf[q, pl.ds(r0, C), slab * LANES:(slab + 1) * LANES] = (
                res[:, :LANES] * _silu(res[:, LANES:]))

        def gb_item(blk):
            res = proj(BLK_GB + blk)
            gbbuf[q, pl.ds(r0, C), blk * NB:(blk + 1) * NB] = _silu(res)

        def conv_unit(j, cl):
            ru = pl.multiple_of(r0 + j * R, R)
            cs = pl.ds(cl * LANES, LANES)
            ls = slice(cl * LANES, (cl + 1) * LANES)
            rows = slice(j * R, (j + 1) * R)
            acc = _rows_at(cabuf, cl, body_b + ru - 1, R) * caw_ref[0:1, cs]
            acc = acc + _rows_at(cabuf, cl, body_b + ru, R) * caw_ref[1:2, cs]
            acc = acc + _rows_at(cabuf, cl, body_b + ru + 1, R) * caw_ref[2:3, cs]
            ybuf[rows, ls] = (acc * gabuf[o, pl.ds(ru, R), ls]).astype(BF16)
            acc = jnp.zeros((R, LANES), F32) + cbb_ref[0:1, cs]
            for b in range(SUBLANES):
                win = _rows_at(ubuf, cl, body_b + ru + b - (CONV_B // 2), R + 3 * SUBLANES)
                for a in range(4):
                    k = SUBLANES * a + b
                    if k < CONV_B:
                        acc = acc + win[SUBLANES * a:SUBLANES * a + R] * cbw_ref[k:k + 1, cs]
            u2buf[rows, ls] = acc

        def headnorm_item(hcol):
            cs = slice(hcol * NB, (hcol + 1) * NB)
            mu = jnp.dot(u2buf[:, cs].astype(BF16), mmat, preferred_element_type=F32)
            dbuf[:, cs] = u2buf[:, cs] - mu
            d = dbuf[:, cs]
            var = jnp.dot((d * d).astype(BF16), mmat, preferred_element_type=F32)
            yn = dbuf[:, cs] * lax.rsqrt(var + LN_EPS) * lng_ref[0:1, cs] + lnb_ref[0:1, cs]
            ybuf[:, W_A + hcol * NB:W_A + (hcol + 1) * NB] = (
                _silu(yn) * gbbuf[o, pl.ds(r0, C), cs]).astype(BF16)

        def out_a_item():
            obuf[...] = jnp.dot(ybuf[:, 0:W_A], w_out_ref[0:W_A, :], preferred_element_type=F32)

        def out_item():
            yb = jnp.dot(ybuf[:, W_A:], w_out_ref[W_A:, :], preferred_element_type=F32)
            obuf[...] = obuf[...] + yb
            for i in range(C // NORM_ROWS):
                rr = pl.multiple_of(r0 + i * NORM_ROWS, NORM_ROWS)
                z = (xbuf[o, pl.ds(rr, NORM_ROWS), :]
                     + gate * obuf[i * NORM_ROWS:(i + 1) * NORM_ROWS, :])
                ms = jnp.mean(z * z, axis=-1, keepdims=True)
                out_ref[pl.ds(rr, NORM_ROWS), :] = z * lax.rsqrt(ms + RMS_EPS) * fg

        def glu(v, g):
            return v * _sigmoid(g)

        def prod(a, b):
            return a * b

        norm_item()
        for slab in range(NSLAB):
            conv_input_item(slab, ubuf, BLK_U, glu)
        for slab in range(NSLAB):
            conv_input_item(slab, cabuf, BLK_CA, prod)
        for slab in range(NSLAB):
            ga_item(slab)
        for blk in range(W_B // NB):
            gb_item(blk)
        for hcol in range(W_B // NB):
            for j in range(C // R):
                for cl in range(hcol * (NB // LANES), (hcol + 1) * (NB // LANES)):
                    conv_unit(j, cl)
        out_a_item()
        for hcol in range(W_B // NB):
            headnorm_item(hcol)
        out_item()
        return carry

    lax.fori_loop(0, nchunk, chunk, 0)


def _mixer_call(x, shift, gs, gate, w_in, caw, cbw, cbb, lng, lnb, mmat, w_out, fg):
    B, L, D = x.shape
    T = SEQ_TILE
    assert L % T == 0 and T % CHUNK == 0 and CHUNK % CONV_ROWS == 0
    nt = L // T
    ntiles = B * nt
    x2 = x.reshape(B * L, D)

    def const(shape):
        return pl.BlockSpec(shape, lambda s: (0,) * len(shape),
                            pipeline_mode=pl.Buffered(1))

    def tile_a(s):
        return jnp.minimum(s, ntiles - 1)

    def tile_b(s):
        return jnp.maximum(s - 1, 0)

    out = pl.pallas_call(
        functools.partial(_mixer_kernel, nt=nt),
        grid=(ntiles + 1,),
        in_specs=[
            pl.BlockSpec((T, D), lambda s: (tile_a(s), 0)),
            pl.BlockSpec((1, 1, D), lambda s: (tile_a(s) // nt, 0, 0)),
            pl.BlockSpec((1, 1, D), lambda s: (tile_a(s) // nt, 0, 0)),
            pl.BlockSpec((1, 1, D), lambda s: (tile_b(s) // nt, 0, 0)),
            const((D, IN_COLS)),
            const((CONV_A, W_A)),
            const((CONV_B, W_B)),
            const((1, W_B)),
            const((1, W_B)),
            const((1, W_B)),
            const((MXU_DIM, MXU_DIM)),
            const((D, D)),
            const((1, D)),
        ],
        out_specs=pl.BlockSpec((T, D), lambda s: (tile_b(s), 0)),
        out_shape=jax.ShapeDtypeStruct((B * L, D), F32),
        scratch_shapes=[
            pltpu.VMEM((CHUNK, D), BF16),
            pltpu.VMEM((2, T, D), F32),
            pltpu.VMEM((NSLAB, 2 * SLOT, LANES), F32),
            pltpu.VMEM((2, T, W_A), F32),
            pltpu.VMEM((NSLAB, 2 * SLOT, LANES), F32),
            pltpu.VMEM((2, T, W_B), F32),
            pltpu.VMEM((CHUNK, D), BF16),
            pltpu.VMEM((CHUNK, W_B), F32),
            pltpu.VMEM((CHUNK, W_B), F32),
            pltpu.VMEM((CHUNK, D), F32),
        ],
        compiler_params=pltpu.CompilerParams(
            dimension_semantics=("arbitrary",), vmem_limit_bytes=VMEM_LIMIT),
        name="mixer_block",
    )(x2, shift, gs, gate, w_in, caw, cbw, cbb, lng, lnb, mmat, w_out, fg)
    return out.reshape(B, L, D)


def _permute_in_cols(w):
    def slab(col, j):
        return w[:, col + j * LANES:col + (j + 1) * LANES]
    parts = []
    for c1, c2 in ((C_BV, C_BG), (C_AIN, C_AC), (C_AB, C_AZ)):
        for j in range(NSLAB):
            parts += [slab(c1, j), slab(c2, j)]
    parts.append(w[:, C_BZ:C_BZ + W_B])
    return jnp.concatenate(parts, axis=1)


def kernel(x_prompt, x_sample, c_prompt, c_sample, norm_g, w_ada, b_ada, w_in, conv_a_w,
           conv_b_w, conv_b_b, ln_g, ln_b, w_out, final_g):
    nb_p = c_prompt.shape[0]
    c_all = jnp.concatenate([c_prompt, c_sample], axis=0)
    mod = _ada_call(c_all, w_ada[0], b_ada[0], norm_g[0])
    mod = mod[:, :, None, :]

    w_in_b = _permute_in_cols(w_in[0]).astype(BF16)
    w_out_b = w_out[0].astype(BF16)
    head = jnp.arange(MXU_DIM) // HEAD_DIM
    mmat = jnp.where(head[:, None] == head[None, :], 1.0 / HEAD_DIM, 0.0).astype(BF16)

    def run(x, lo, hi):
        return _mixer_call(x, mod[0, lo:hi], mod[1, lo:hi], mod[2, lo:hi], w_in_b,
                           conv_a_w[0], conv_b_w[0], conv_b_b[0].reshape(1, -1),
                           ln_g[0].reshape(1, -1), ln_b[0].reshape(1, -1), mmat, w_out_b,
                           final_g.reshape(1, -1))

    y_prompt = run(x_prompt, 0, nb_p)
    y_sample = run(x_sample, nb_p, c_all.shape[0])
    return (y_prompt, y_sample)
```

```python
import functools

import jax
import jax.numpy as jnp
from jax import lax
from jax.experimental import pallas as pl
from jax.experimental.pallas import tpu as pltpu

D_MODEL = 1024
W_A = 512
W_B = 512
HEAD_DIM = 64
CONV_A = 3
CONV_B = 31
RMS_EPS = 1e-6
LN_EPS = 1e-5
IN_COLS = 4 * W_A + 3 * W_B

C_AIN, C_AB, C_AC, C_AZ, C_BV, C_BG, C_BZ = (0, 512, 1024, 1536, 2048, 2560, 3072)
BLK_U, BLK_CA, BLK_GA, BLK_GB = 0, 4, 8, 12

LANES = 128
SUBLANES = 8
MXU_DIM = 256
HALO = 16
SEQ_TILE = 1024
CHUNK = 512
CONV_ROWS = 64
NORM_ROWS = 32
SLOT = SEQ_TILE + 2 * HALO
NSLAB = W_A // LANES
VMEM_LIMIT = 58 * 1024 * 1024

F32 = jnp.float32
BF16 = jnp.bfloat16


def _sigmoid(x):
    return 1.0 / (1.0 + jnp.exp(-x))


def _silu(x):
    return x * _sigmoid(x)


def _ada_kernel(c_ref, w_ref, b_ref, ng_ref, out_ref):
    j = pl.program_id(0)
    c = c_ref[...]
    s = _silu(c)
    mod = jnp.dot(s, w_ref[...], precision=lax.Precision.HIGHEST,
                  preferred_element_type=F32) + b_ref[...]
    out_ref[0] = jnp.where(j == 1, ng_ref[...] * (1.0 + mod), mod)


def _ada_call(c_all, w_ada, b_ada, norm_g):
    n = c_all.shape[0]
    return pl.pallas_call(
        _ada_kernel,
        grid=(3,),
        in_specs=[
            pl.BlockSpec((n, D_MODEL), lambda j: (0, 0)),
            pl.BlockSpec((D_MODEL, D_MODEL), lambda j: (0, j)),
            pl.BlockSpec((1, D_MODEL), lambda j: (0, j)),
            pl.BlockSpec((1, D_MODEL), lambda j: (0, 0)),
        ],
        out_specs=pl.BlockSpec((1, n, D_MODEL), lambda j: (j, 0, 0)),
        out_shape=jax.ShapeDtypeStruct((3, n, D_MODEL), F32),
        compiler_params=pltpu.CompilerParams(
            dimension_semantics=("arbitrary",), vmem_limit_bytes=VMEM_LIMIT),
        name="adaln_mod",
    )(c_all, w_ada, b_ada.reshape(1, -1), norm_g.reshape(1, -1))


def _rows_at(buf, slab, start, n):
    return buf[pl.ds(slab, 1, stride=2), pl.ds(start, n), :][0]


def _mixer_kernel(x_ref, shift_ref, gs_ref, gate_ref, w_in_ref,
                  caw_ref, cbw_ref, cbb_ref, lng_ref, lnb_ref, m_ref, w_out_ref, fg_ref,
                  out_ref, hbuf, xbuf, cabuf, gabuf, ubuf, gbbuf, ybuf, u2buf, dbuf, obuf,
                  *, nt):
    T, C, R = SEQ_TILE, CHUNK, CONV_ROWS
    NB = MXU_DIM
    nchunk = T // C
    s = pl.program_id(0)
    first_a = (s % nt) == 0
    last_a = (s % nt) == nt - 1

    @pl.when(s == 0)
    def _():
        xbuf[...] = jnp.zeros(xbuf.shape, F32)
        cabuf[...] = jnp.zeros(cabuf.shape, F32)
        gabuf[...] = jnp.zeros(gabuf.shape, F32)
        ubuf[...] = jnp.zeros(ubuf.shape, F32)
        gbbuf[...] = jnp.zeros(gbbuf.shape, F32)

    q = s % 2
    o = 1 - q
    body_a = pl.multiple_of(q * SLOT + HALO, HALO)
    body_b = pl.multiple_of(o * SLOT + HALO, HALO)
    pt_o = pl.multiple_of(o * SLOT, HALO)
    nh_o = pl.multiple_of(o * SLOT + HALO + T, HALO)

    def chunk(c, carry):
        r0 = pl.multiple_of(c * C, C)
        shift = shift_ref[0]
        gs = gs_ref[0]
        gate = gate_ref[0]
        fg = fg_ref[...]
        mmat = m_ref[...]

        def proj(blk):
            return jnp.dot(hbuf[...], w_in_ref[:, blk * NB:(blk + 1) * NB],
                           preferred_element_type=F32)

        def norm_item():
            for i in range(C // NORM_ROWS):
                rr = pl.multiple_of(r0 + i * NORM_ROWS, NORM_ROWS)
                x = x_ref[pl.ds(rr, NORM_ROWS), :]
                xbuf[q, pl.ds(rr, NORM_ROWS), :] = x
                ms = jnp.mean(x * x, axis=-1, keepdims=True)
                hbuf[i * NORM_ROWS:(i + 1) * NORM_ROWS, :] = (
                    (x * lax.rsqrt(ms + RMS_EPS)) * gs + shift).astype(BF16)

        def conv_input_item(slab, buf, blk0, combine):
            res = proj(blk0 + slab)
            val = combine(res[:, :LANES], res[:, LANES:])
            buf[slab, pl.ds(body_a + r0, C), :] = val
            head = jnp.where(first_a, 0.0, val[0:HALO])
            buf[slab, pl.ds(nh_o, HALO), :] = jnp.where(
                c == 0, head, buf[slab, pl.ds(nh_o, HALO), :])
            tail = jnp.where(last_a, 0.0, val[C - HALO:C])
            buf[slab, pl.ds(pt_o, HALO), :] = jnp.where(
                c == nchunk - 1, tail, buf[slab, pl.ds(pt_o, HALO), :])

        def ga_item(slab):
            res = proj(BLK_GA + slab)
            gabuf[q, pl.ds(r0, C), slab * LANES:(slab + 1) * LANES] = (
                res[:, :LANES] * _silu(res[:, LANES:]))

        def gb_item(blk):
            res = proj(BLK_GB + blk)
            gbbuf[q, pl.ds(r0, C), blk * NB:(blk + 1) * NB] = _silu(res)

        def conv_unit(j, cl):
            ru = pl.multiple_of(r0 + j * R, R)
            cs = pl.ds(cl * LANES, LANES)
            ls = slice(cl * LANES, (cl + 1) * LANES)
            rows = slice(j * R, (j + 1) * R)
            acc = _rows_at(cabuf, cl, body_b + ru - 1, R) * caw_ref[0:1, cs]
            acc = acc + _rows_at(cabuf, cl, body_b + ru, R) * caw_ref[1:2, cs]
            acc = acc + _rows_at(cabuf, cl, body_b + ru + 1, R) * caw_ref[2:3, cs]
            ybuf[rows, ls] = (acc * gabuf[o, pl.ds(ru, R), ls]).astype(BF16)
            acc = jnp.zeros((R, LANES), F32) + cbb_ref[0:1, cs]
            for b in range(SUBLANES):
                win = _rows_at(ubuf, cl, body_b + ru + b - (CONV_B // 2), R + 3 * SUBLANES)
                for a in range(4):
                    k = SUBLANES * a + b
                    if k < CONV_B:
                        acc = acc + win[SUBLANES * a:SUBLANES * a + R] * cbw_ref[k:k + 1, cs]
            u2buf[rows, ls] = acc

        def headnorm_item(hcol):
            cs = slice(hcol * NB, (hcol + 1) * NB)
            mu = jnp.dot(u2buf[:, cs].astype(BF16), mmat, preferred_element_type=F32)
            dbuf[:, cs] = u2buf[:, cs] - mu
            d = dbuf[:, cs]
            var = jnp.dot((d * d).astype(BF16), mmat, preferred_element_type=F32)
            yn = dbuf[:, cs] * lax.rsqrt(var + LN_EPS) * lng_ref[0:1, cs] + lnb_ref[0:1, cs]
            ybuf[:, W_A + hcol * NB:W_A + (hcol + 1) * NB] = (
                _silu(yn) * gbbuf[o, pl.ds(r0, C), cs]).astype(BF16)

        def out_a_item():
            obuf[...] = jnp.dot(ybuf[:, 0:W_A], w_out_ref[0:W_A, :], preferred_element_type=F32)

        def out_item():
            yb = jnp.dot(ybuf[:, W_A:], w_out_ref[W_A:, :], preferred_element_type=F32)
            obuf[...] = obuf[...] + yb
            for i in range(C // NORM_ROWS):
                rr = pl.multiple_of(r0 + i * NORM_ROWS, NORM_ROWS)
                z = (xbuf[o, pl.ds(rr, NORM_ROWS), :]
                     + gate * obuf[i * NORM_ROWS:(i + 1) * NORM_ROWS, :])
                ms = jnp.mean(z * z, axis=-1, keepdims=True)
                out_ref[pl.ds(rr, NORM_ROWS), :] = z * lax.rsqrt(ms + RMS_EPS) * fg

        def glu(v, g):
            return v * _sigmoid(g)

        def prod(a, b):
            return a * b

        norm_item()
        for slab in range(NSLAB):
            conv_input_item(slab, ubuf, BLK_U, glu)
        for slab in range(NSLAB):
            conv_input_item(slab, cabuf, BLK_CA, prod)
        for slab in range(NSLAB):
            ga_item(slab)
        for blk in range(W_B // NB):
            gb_item(blk)
        for hcol in range(W_B // NB):
            for j in range(C // R):
                for cl in range(hcol * (NB // LANES), (hcol + 1) * (NB // LANES)):
                    conv_unit(j, cl)
        out_a_item()
        for hcol in range(W_B // NB):
            headnorm_item(hcol)
        out_item()
        return carry

    lax.fori_loop(0, nchunk, chunk, 0)


def _mixer_call(x, shift, gs, gate, w_in, caw, cbw, cbb, lng, lnb, mmat, w_out, fg):
    B, L, D = x.shape
    T = SEQ_TILE
    assert L % T == 0 and T % CHUNK == 0 and CHUNK % CONV_ROWS == 0
    nt = L // T
    ntiles = B * nt
    x2 = x.reshape(B * L, D)

    def const(shape):
        return pl.BlockSpec(shape, lambda s: (0,) * len(shape),
                            pipeline_mode=pl.Buffered(1))

    def tile_a(s):
        return jnp.minimum(s, ntiles - 1)

    def tile_b(s):
        return jnp.maximum(s - 1, 0)

    out = pl.pallas_call(
        functools.partial(_mixer_kernel, nt=nt),
        grid=(ntiles + 1,),
        in_specs=[
            pl.BlockSpec((T, D), lambda s: (tile_a(s), 0)),
            pl.BlockSpec((1, 1, D), lambda s: (tile_a(s) // nt, 0, 0)),
            pl.BlockSpec((1, 1, D), lambda s: (tile_a(s) // nt, 0, 0)),
            pl.BlockSpec((1, 1, D), lambda s: (tile_b(s) // nt, 0, 0)),
            const((D, IN_COLS)),
            const((CONV_A, W_A)),
            const((CONV_B, W_B)),
            const((1, W_B)),
            const((1, W_B)),
            const((1, W_B)),
            const((MXU_DIM, MXU_DIM)),
            const((D, D)),
            const((1, D)),
        ],
        out_specs=pl.BlockSpec((T, D), lambda s: (tile_b(s), 0)),
        out_shape=jax.ShapeDtypeStruct((B * L, D), F32),
        scratch_shapes=[
            pltpu.VMEM((CHUNK, D), BF16),
            pltpu.VMEM((2, T, D), F32),
            pltpu.VMEM((NSLAB, 2 * SLOT, LANES), F32),
            pltpu.VMEM((2, T, W_A), F32),
            pltpu.VMEM((NSLAB, 2 * SLOT, LANES), F32),
            pltpu.VMEM((2, T, W_B), F32),
            pltpu.VMEM((CHUNK, D), BF16),
            pltpu.VMEM((CHUNK, W_B), F32),
            pltpu.VMEM((CHUNK, W_B), F32),
            pltpu.VMEM((CHUNK, D), F32),
        ],
        compiler_params=pltpu.CompilerParams(
            dimension_semantics=("arbitrary",), vmem_limit_bytes=VMEM_LIMIT),
        name="mixer_block",
    )(x2, shift, gs, gate, w_in, caw, cbw, cbb, lng, lnb, mmat, w_out, fg)
    return out.reshape(B, L, D)


def _permute_in_cols(w):
    def slab(col, j):
        return w[:, col + j * LANES:col + (j + 1) * LANES]
    parts = []
    for c1, c2 in ((C_BV, C_BG), (C_AIN, C_AC), (C_AB, C_AZ)):
        for j in range(NSLAB):
            parts += [slab(c1, j), slab(c2, j)]
    parts.append(w[:, C_BZ:C_BZ + W_B])
    return jnp.concatenate(parts, axis=1)


def kernel(x_prompt, x_sample, c_prompt, c_sample, norm_g, w_ada, b_ada, w_in, conv_a_w,
           conv_b_w, conv_b_b, ln_g, ln_b, w_out, final_g):
    nb_p = c_prompt.shape[0]
    c_all = jnp.concatenate([c_prompt, c_sample], axis=0)
    mod = _ada_call(c_all, w_ada[0], b_ada[0], norm_g[0])
    mod = mod[:, :, None, :]

    w_in_b = _permute_in_cols(w_in[0]).astype(BF16)
    w_out_b = w_out[0].astype(BF16)
    head = jnp.arange(MXU_DIM) // HEAD_DIM
    mmat = jnp.where(head[:, None] == head[None, :], 1.0 / HEAD_DIM, 0.0).astype(BF16)

    def run(x, lo, hi):
        return _mixer_call(x, mod[0, lo:hi], mod[1, lo:hi], mod[2, lo:hi], w_in_b,
                           conv_a_w[0], conv_b_w[0], conv_b_b[0].reshape(1, -1),
                           ln_g[0].reshape(1, -1), ln_b[0].reshape(1, -1), mmat, w_out_b,
                           final_g.reshape(1, -1))

    y_prompt = run(x_prompt, 0, nb_p)
    y_sample = run(x_sample, nb_p, c_all.shape[0])
    return (y_prompt, y_sample)
```

```python
import functools

import jax
import jax.numpy as jnp
from jax import lax
from jax.experimental import pallas as pl
from jax.experimental.pallas import tpu as pltpu

D_MODEL = 1024
W_A = 512
W_B = 512
HEAD_DIM = 64
CONV_A = 3
CONV_B = 31
RMS_EPS = 1e-6
LN_EPS = 1e-5
IN_COLS = 4 * W_A + 3 * W_B

C_AIN, C_AB, C_AC, C_AZ, C_BV, C_BG, C_BZ = (0, 512, 1024, 1536, 2048, 2560, 3072)
BLK_U, BLK_CA, BLK_GA, BLK_GB = 0, 4, 8, 12

LANES = 128
SUBLANES = 8
MXU_DIM = 256
HALO = 16
SEQ_TILE = 1024
CHUNK = 512
CONV_ROWS = 64
NORM_ROWS = 32
SLOT = SEQ_TILE + 2 * HALO
NSLAB = W_A // LANES
VMEM_LIMIT = 58 * 1024 * 1024

F32 = jnp.float32
BF16 = jnp.bfloat16


NEG_LOG2_E = -1.4426950408889634


def _sigmoid(x):
    return 1.0 / (1.0 + jnp.exp2(x * NEG_LOG2_E))


def _silu(x):
    return x * _sigmoid(x)


def _ada_kernel(c_ref, w_ref, b_ref, ng_ref, out_ref):
    j = pl.program_id(0)
    c = c_ref[...]
    s = _silu(c)
    mod = jnp.dot(s, w_ref[...], precision=lax.Precision.HIGHEST,
                  preferred_element_type=F32) + b_ref[...]
    out_ref[0] = jnp.where(j == 1, ng_ref[...] * (1.0 + mod), mod)


def _ada_call(c_all, w_ada, b_ada, norm_g):
    n = c_all.shape[0]
    return pl.pallas_call(
        _ada_kernel,
        grid=(3,),
        in_specs=[
            pl.BlockSpec((n, D_MODEL), lambda j: (0, 0)),
            pl.BlockSpec((D_MODEL, D_MODEL), lambda j: (0, j)),
            pl.BlockSpec((1, D_MODEL), lambda j: (0, j)),
            pl.BlockSpec((1, D_MODEL), lambda j: (0, 0)),
        ],
        out_specs=pl.BlockSpec((1, n, D_MODEL), lambda j: (j, 0, 0)),
        out_shape=jax.ShapeDtypeStruct((3, n, D_MODEL), F32),
        compiler_params=pltpu.CompilerParams(
            dimension_semantics=("arbitrary",), vmem_limit_bytes=VMEM_LIMIT),
        name="adaln_mod",
    )(c_all, w_ada, b_ada.reshape(1, -1), norm_g.reshape(1, -1))


def _rows_at(buf, slab, start, n):
    return buf[pl.ds(slab, 1, stride=2), pl.ds(start, n), :][0]


def _mixer_kernel(x_ref, shift_ref, gs_ref, gate_ref, w_in_ref,
                  caw_ref, cbw_ref, cbb_ref, lng_ref, lnb_ref, m_ref, w_out_ref, fg_ref,
                  out_ref, hbuf, xbuf, cabuf, gabuf, ubuf, gbbuf, ybuf, u2buf, dbuf, obuf,
                  *, nt):
    T, C, R = SEQ_TILE, CHUNK, CONV_ROWS
    NB = MXU_DIM
    nchunk = T // C
    s = pl.program_id(0)
    first_a = (s % nt) == 0
    last_a = (s % nt) == nt - 1

    @pl.when(s == 0)
    def _():
        xbuf[...] = jnp.zeros(xbuf.shape, F32)
        cabuf[...] = jnp.zeros(cabuf.shape, F32)
        gabuf[...] = jnp.zeros(gabuf.shape, F32)
        ubuf[...] = jnp.zeros(ubuf.shape, F32)
        gbbuf[...] = jnp.zeros(gbbuf.shape, F32)

    q = s % 2
    o = 1 - q
    body_a = pl.multiple_of(q * SLOT + HALO, HALO)
    body_b = pl.multiple_of(o * SLOT + HALO, HALO)
    pt_o = pl.multiple_of(o * SLOT, HALO)
    nh_o = pl.multiple_of(o * SLOT + HALO + T, HALO)

    def chunk(c, carry):
        r0 = pl.multiple_of(c * C, C)
        shift = shift_ref[0]
        gs = gs_ref[0]
        gate = gate_ref[0]
        fg = fg_ref[...]
        mmat = m_ref[...]

        def proj(blk):
            return jnp.dot(hbuf[...], w_in_ref[:, blk * NB:(blk + 1) * NB],
                           preferred_element_type=F32)

        def norm_item():
            for i in range(C // NORM_ROWS):
                rr = pl.multiple_of(r0 + i * NORM_ROWS, NORM_ROWS)
                x = x_ref[pl.ds(rr, NORM_ROWS), :]
                xbuf[q, pl.ds(rr, NORM_ROWS), :] = x
                ms = jnp.mean(x * x, axis=-1, keepdims=True)
                hbuf[i * NORM_ROWS:(i + 1) * NORM_ROWS, :] = (
                    (x * lax.rsqrt(ms + RMS_EPS)) * gs + shift).astype(BF16)

        def conv_input_item(slab, buf, blk0, combine):
            res = proj(blk0 + slab)
            val = combine(res[:, :LANES], res[:, LANES:])
            buf[slab, pl.ds(body_a + r0, C), :] = val
            head = jnp.where(first_a, 0.0, val[0:HALO])
            buf[slab, pl.ds(nh_o, HALO), :] = jnp.where(
                c == 0, head, buf[slab, pl.ds(nh_o, HALO), :])
            tail = jnp.where(last_a, 0.0, val[C - HALO:C])
            buf[slab, pl.ds(pt_o, HALO), :] = jnp.where(
                c == nchunk - 1, tail, buf[slab, pl.ds(pt_o, HALO), :])

        def ga_item(slab):
            res = proj(BLK_GA + slab)
            gabuf[q, pl.ds(r0, C), slab * LANES:(slab + 1) * LANES] = (
                res[:, :LANES] * _silu(res[:, LANES:]))

        def gb_item(blk):
            res = proj(BLK_GB + blk)
            gbbuf[q, pl.ds(r0, C), blk * NB:(blk + 1) * NB] = _silu(res)

        def conv_unit(j, cl):
            ru = pl.multiple_of(r0 + j * R, R)
            cs = pl.ds(cl * LANES, LANES)
            ls = slice(cl * LANES, (cl + 1) * LANES)
            rows = slice(j * R, (j + 1) * R)
            acc = _rows_at(cabuf, cl, body_b + ru - 1, R) * caw_ref[0:1, cs]
            acc = acc + _rows_at(cabuf, cl, body_b + ru, R) * caw_ref[1:2, cs]
            acc = acc + _rows_at(cabuf, cl, body_b + ru + 1, R) * caw_ref[2:3, cs]
            ybuf[rows, ls] = (acc * gabuf[o, pl.ds(ru, R), ls]).astype(BF16)
            acc = jnp.zeros((R, LANES), F32) + cbb_ref[0:1, cs]
            for b in range(SUBLANES):
                win = _rows_at(ubuf, cl, body_b + ru + b - (CONV_B // 2), R + 3 * SUBLANES)
                for a in range(4):
                    k = SUBLANES * a + b
                    if k < CONV_B:
                        acc = acc + win[SUBLANES * a:SUBLANES * a + R] * cbw_ref[k:k + 1, cs]
            u2buf[rows, ls] = acc

        def headnorm_item(hcol):
            cs = slice(hcol * NB, (hcol + 1) * NB)
            mu = jnp.dot(u2buf[:, cs].astype(BF16), mmat, preferred_element_type=F32)
            dbuf[:, cs] = u2buf[:, cs] - mu
            d = dbuf[:, cs]
            var = jnp.dot((d * d).astype(BF16), mmat, preferred_element_type=F32)
            yn = dbuf[:, cs] * lax.rsqrt(var + LN_EPS) * lng_ref[0:1, cs] + lnb_ref[0:1, cs]
            ybuf[:, W_A + hcol * NB:W_A + (hcol + 1) * NB] = (
                _silu(yn) * gbbuf[o, pl.ds(r0, C), cs]).astype(BF16)

        def out_a_item():
            obuf[...] = jnp.dot(ybuf[:, 0:W_A], w_out_ref[0:W_A, :], preferred_element_type=F32)

        def out_item():
            yb = jnp.dot(ybuf[:, W_A:], w_out_ref[W_A:, :], preferred_element_type=F32)
            obuf[...] = obuf[...] + yb
            for i in range(C // NORM_ROWS):
                rr = pl.multiple_of(r0 + i * NORM_ROWS, NORM_ROWS)
                z = (xbuf[o, pl.ds(rr, NORM_ROWS), :]
                     + gate * obuf[i * NORM_ROWS:(i + 1) * NORM_ROWS, :])
                ms = jnp.mean(z * z, axis=-1, keepdims=True)
                out_ref[pl.ds(rr, NORM_ROWS), :] = z * lax.rsqrt(ms + RMS_EPS) * fg

        def glu(v, g):
            return v * _sigmoid(g)

        def prod(a, b):
            return a * b

        norm_item()
        for slab in range(NSLAB):
            conv_input_item(slab, ubuf, BLK_U, glu)
        for slab in range(NSLAB):
            conv_input_item(slab, cabuf, BLK_CA, prod)
        for slab in range(NSLAB):
            ga_item(slab)
        for blk in range(W_B // NB):
            gb_item(blk)
        for hcol in range(W_B // NB):
            for j in range(C // R):
                for cl in range(hcol * (NB // LANES), (hcol + 1) * (NB // LANES)):
                    conv_unit(j, cl)
        out_a_item()
        for hcol in range(W_B // NB):
            headnorm_item(hcol)
        out_item()
        return carry

    lax.fori_loop(0, nchunk, chunk, 0)


def _mixer_call(x, shift, gs, gate, w_in, caw, cbw, cbb, lng, lnb, mmat, w_out, fg):
    B, L, D = x.shape
    T = SEQ_TILE
    assert L % T == 0 and T % CHUNK == 0 and CHUNK % CONV_ROWS == 0
    nt = L // T
    ntiles = B * nt
    x2 = x.reshape(B * L, D)

    def const(shape):
        return pl.BlockSpec(shape, lambda s: (0,) * len(shape),
                            pipeline_mode=pl.Buffered(1))

    def tile_a(s):
        return jnp.minimum(s, ntiles - 1)

    def tile_b(s):
        return jnp.maximum(s - 1, 0)

    out = pl.pallas_call(
        functools.partial(_mixer_kernel, nt=nt),
        grid=(ntiles + 1,),
        in_specs=[
            pl.BlockSpec((T, D), lambda s: (tile_a(s), 0)),
            pl.BlockSpec((1, 1, D), lambda s: (tile_a(s) // nt, 0, 0)),
            pl.BlockSpec((1, 1, D), lambda s: (tile_a(s) // nt, 0, 0)),
            pl.BlockSpec((1, 1, D), lambda s: (tile_b(s) // nt, 0, 0)),
            const((D, IN_COLS)),
            const((CONV_A, W_A)),
            const((CONV_B, W_B)),
            const((1, W_B)),
            const((1, W_B)),
            const((1, W_B)),
            const((MXU_DIM, MXU_DIM)),
            const((D, D)),
            const((1, D)),
        ],
        out_specs=pl.BlockSpec((T, D), lambda s: (tile_b(s), 0)),
        out_shape=jax.ShapeDtypeStruct((B * L, D), F32),
        scratch_shapes=[
            pltpu.VMEM((CHUNK, D), BF16),
            pltpu.VMEM((2, T, D), F32),
            pltpu.VMEM((NSLAB, 2 * SLOT, LANES), F32),
            pltpu.VMEM((2, T, W_A), F32),
            pltpu.VMEM((NSLAB, 2 * SLOT, LANES), F32),
            pltpu.VMEM((2, T, W_B), F32),
            pltpu.VMEM((CHUNK, D), BF16),
            pltpu.VMEM((CHUNK, W_B), F32),
            pltpu.VMEM((CHUNK, W_B), F32),
            pltpu.VMEM((CHUNK, D), F32),
        ],
        compiler_params=pltpu.CompilerParams(
            dimension_semantics=("arbitrary",), vmem_limit_bytes=VMEM_LIMIT),
        name="mixer_block",
    )(x2, shift, gs, gate, w_in, caw, cbw, cbb, lng, lnb, mmat, w_out, fg)
    return out.reshape(B, L, D)


def _permute_in_cols(w):
    def slab(col, j):
        return w[:, col + j * LANES:col + (j + 1) * LANES]
    parts = []
    for c1, c2 in ((C_BV, C_BG), (C_AIN, C_AC), (C_AB, C_AZ)):
        for j in range(NSLAB):
            parts += [slab(c1, j), slab(c2, j)]
    parts.append(w[:, C_BZ:C_BZ + W_B])
    return jnp.concatenate(parts, axis=1)


def kernel(x_prompt, x_sample, c_prompt, c_sample, norm_g, w_ada, b_ada, w_in, conv_a_w,
           conv_b_w, conv_b_b, ln_g, ln_b, w_out, final_g):
    nb_p = c_prompt.shape[0]
    c_all = jnp.concatenate([c_prompt, c_sample], axis=0)
    mod = _ada_call(c_all, w_ada[0], b_ada[0], norm_g[0])
    mod = mod[:, :, None, :]

    w_in_b = _permute_in_cols(w_in[0]).astype(BF16)
    w_out_b = w_out[0].astype(BF16)
    head = jnp.arange(MXU_DIM) // HEAD_DIM
    mmat = jnp.where(head[:, None] == head[None, :], 1.0 / HEAD_DIM, 0.0).astype(BF16)

    def run(x, lo, hi):
        return _mixer_call(x, mod[0, lo:hi], mod[1, lo:hi], mod[2, lo:hi], w_in_b,
                           conv_a_w[0], conv_b_w[0], conv_b_b[0].reshape(1, -1),
                           ln_g[0].reshape(1, -1), ln_b[0].reshape(1, -1), mmat, w_out_b,
                           final_g.reshape(1, -1))

    y_prompt = run(x_prompt, 0, nb_p)
    y_sample = run(x_sample, nb_p, c_all.shape[0])
    return (y_prompt, y_sample)
```

```python
import functools

import jax
import jax.numpy as jnp
from jax import lax
from jax.experimental import pallas as pl
from jax.experimental.pallas import tpu as pltpu

D_MODEL = 1024
W_A = 512
W_B = 512
HEAD_DIM = 64
CONV_A = 3
CONV_B = 31
RMS_EPS = 1e-6
LN_EPS = 1e-5
IN_COLS = 4 * W_A + 3 * W_B

C_AIN, C_AB, C_AC, C_AZ, C_BV, C_BG, C_BZ = (0, 512, 1024, 1536, 2048, 2560, 3072)
BLK_U, BLK_CA, BLK_GA, BLK_GB = 0, 4, 8, 12

LANES = 128
SUBLANES = 8
MXU_DIM = 256
HALO = 16
SEQ_TILE = 1024
CHUNK = 512
CONV_ROWS = 64
NORM_ROWS = 32
SLOT = SEQ_TILE + 2 * HALO
NSLAB = W_A // LANES
VMEM_LIMIT = 58 * 1024 * 1024

F32 = jnp.float32
BF16 = jnp.bfloat16


NEG_LOG2_E = -1.4426950408889634


def _sigmoid(x):
    return 1.0 / (1.0 + jnp.exp2(x * NEG_LOG2_E))


def _silu(x):
    return x * _sigmoid(x)


def _ada_kernel(c_ref, w_ref, b_ref, ng_ref, out_ref):
    j = pl.program_id(0)
    c = c_ref[...]
    s = _silu(c)
    mod = jnp.dot(s, w_ref[...], precision=lax.Precision.HIGHEST,
                  preferred_element_type=F32) + b_ref[...]
    out_ref[0] = jnp.where(j == 1, ng_ref[...] * (1.0 + mod), mod)


def _ada_call(c_all, w_ada, b_ada, norm_g):
    n = c_all.shape[0]
    return pl.pallas_call(
        _ada_kernel,
        grid=(3,),
        in_specs=[
            pl.BlockSpec((n, D_MODEL), lambda j: (0, 0)),
            pl.BlockSpec((D_MODEL, D_MODEL), lambda j: (0, j)),
            pl.BlockSpec((1, D_MODEL), lambda j: (0, j)),
            pl.BlockSpec((1, D_MODEL), lambda j: (0, 0)),
        ],
        out_specs=pl.BlockSpec((1, n, D_MODEL), lambda j: (j, 0, 0)),
        out_shape=jax.ShapeDtypeStruct((3, n, D_MODEL), F32),
        compiler_params=pltpu.CompilerParams(
            dimension_semantics=("arbitrary",), vmem_limit_bytes=VMEM_LIMIT),
        name="adaln_mod",
    )(c_all, w_ada, b_ada.reshape(1, -1), norm_g.reshape(1, -1))


def _rows_at(buf, slab, start, n):
    return buf[pl.ds(slab, 1, stride=2), pl.ds(start, n), :][0]


def _mixer_kernel(x_ref, xo_ref, shift_ref, gs_ref, gate_ref, w_in_ref,
                  caw_ref, cbw_ref, cbb_ref, lng_ref, lnb_ref, m_ref, w_out_ref, fg_ref,
                  out_ref, hbuf, cabuf, gabuf, ubuf, gbbuf, ybuf, u2buf, dbuf, obuf,
                  *, nt):
    T, C, R = SEQ_TILE, CHUNK, CONV_ROWS
    NB = MXU_DIM
    nchunk = T // C
    s = pl.program_id(0)
    first_a = (s % nt) == 0
    last_a = (s % nt) == nt - 1

    @pl.when(s == 0)
    def _():
        cabuf[...] = jnp.zeros(cabuf.shape, F32)
        gabuf[...] = jnp.zeros(gabuf.shape, F32)
        ubuf[...] = jnp.zeros(ubuf.shape, F32)
        gbbuf[...] = jnp.zeros(gbbuf.shape, F32)

    q = s % 2
    o = 1 - q
    body_a = pl.multiple_of(q * SLOT + HALO, HALO)
    body_b = pl.multiple_of(o * SLOT + HALO, HALO)
    pt_o = pl.multiple_of(o * SLOT, HALO)
    nh_o = pl.multiple_of(o * SLOT + HALO + T, HALO)

    def chunk(c, carry):
        r0 = pl.multiple_of(c * C, C)
        shift = shift_ref[0]
        gs = gs_ref[0]
        gate = gate_ref[0]
        fg = fg_ref[...]
        mmat = m_ref[...]

        def proj(blk):
            return jnp.dot(hbuf[...], w_in_ref[:, blk * NB:(blk + 1) * NB],
                           preferred_element_type=F32)

        def norm_item():
            for i in range(C // NORM_ROWS):
                rr = pl.multiple_of(r0 + i * NORM_ROWS, NORM_ROWS)
                x = x_ref[pl.ds(rr, NORM_ROWS), :]
                ms = jnp.mean(x * x, axis=-1, keepdims=True)
                hbuf[i * NORM_ROWS:(i + 1) * NORM_ROWS, :] = (
                    (x * lax.rsqrt(ms + RMS_EPS)) * gs + shift).astype(BF16)

        def conv_input_item(slab, buf, blk0, combine):
            res = proj(blk0 + slab)
            val = combine(res[:, :LANES], res[:, LANES:])
            buf[slab, pl.ds(body_a + r0, C), :] = val
            head = jnp.where(first_a, 0.0, val[0:HALO])
            buf[slab, pl.ds(nh_o, HALO), :] = jnp.where(
                c == 0, head, buf[slab, pl.ds(nh_o, HALO), :])
            tail = jnp.where(last_a, 0.0, val[C - HALO:C])
            buf[slab, pl.ds(pt_o, HALO), :] = jnp.where(
                c == nchunk - 1, tail, buf[slab, pl.ds(pt_o, HALO), :])

        def ga_item(slab):
            res = proj(BLK_GA + slab)
            gabuf[q, pl.ds(r0, C), slab * LANES:(slab + 1) * LANES] = (
                res[:, :LANES] * _silu(res[:, LANES:]))

        def gb_item(blk):
            res = proj(BLK_GB + blk)
            gbbuf[q, pl.ds(r0, C), blk * NB:(blk + 1) * NB] = _silu(res)

        def conv_unit(j, cl):
            ru = pl.multiple_of(r0 + j * R, R)
            cs = pl.ds(cl * LANES, LANES)
            ls = slice(cl * LANES, (cl + 1) * LANES)
            rows = slice(j * R, (j + 1) * R)
            acc = _rows_at(cabuf, cl, body_b + ru - 1, R) * caw_ref[0:1, cs]
            acc = acc + _rows_at(cabuf, cl, body_b + ru, R) * caw_ref[1:2, cs]
            acc = acc + _rows_at(cabuf, cl, body_b + ru + 1, R) * caw_ref[2:3, cs]
            ybuf[rows, ls] = (acc * gabuf[o, pl.ds(ru, R), ls]).astype(BF16)
            acc = jnp.zeros((R, LANES), F32) + cbb_ref[0:1, cs]
            for b in range(SUBLANES):
                win = _rows_at(ubuf, cl, body_b + ru + b - (CONV_B // 2), R + 3 * SUBLANES)
                for a in range(4):
                    k = SUBLANES * a + b
                    if k < CONV_B:
                        acc = acc + win[SUBLANES * a:SUBLANES * a + R] * cbw_ref[k:k + 1, cs]
            u2buf[rows, ls] = acc

        def headnorm_item(hcol):
            cs = slice(hcol * NB, (hcol + 1) * NB)
            mu = jnp.dot(u2buf[:, cs].astype(BF16), mmat, preferred_element_type=F32)
            dbuf[:, cs] = u2buf[:, cs] - mu
            d = dbuf[:, cs]
            var = jnp.dot((d * d).astype(BF16), mmat, preferred_element_type=F32)
            yn = dbuf[:, cs] * lax.rsqrt(var + LN_EPS) * lng_ref[0:1, cs] + lnb_ref[0:1, cs]
            ybuf[:, W_A + hcol * NB:W_A + (hcol + 1) * NB] = (
                _silu(yn) * gbbuf[o, pl.ds(r0, C), cs]).astype(BF16)

        def out_a_item():
            obuf[...] = jnp.dot(ybuf[:, 0:W_A], w_out_ref[0:W_A, :], preferred_element_type=F32)

        def out_item():
            yb = jnp.dot(ybuf[:, W_A:], w_out_ref[W_A:, :], preferred_element_type=F32)
            obuf[...] = obuf[...] + yb
            for i in range(C // NORM_ROWS):
                rr = pl.multiple_of(r0 + i * NORM_ROWS, NORM_ROWS)
                z = (xo_ref[pl.ds(rr, NORM_ROWS), :]
                     + gate * obuf[i * NORM_ROWS:(i + 1) * NORM_ROWS, :])
                ms = jnp.mean(z * z, axis=-1, keepdims=True)
                out_ref[pl.ds(rr, NORM_ROWS), :] = z * lax.rsqrt(ms + RMS_EPS) * fg

        def glu(v, g):
            return v * _sigmoid(g)

        def prod(a, b):
            return a * b

        norm_item()
        for slab in range(NSLAB):
            conv_input_item(slab, ubuf, BLK_U, glu)
        for slab in range(NSLAB):
            conv_input_item(slab, cabuf, BLK_CA, prod)
        for slab in range(NSLAB):
            ga_item(slab)
        for blk in range(W_B // NB):
            gb_item(blk)
        for hcol in range(W_B // NB):
            for j in range(C // R):
                for cl in range(hcol * (NB // LANES), (hcol + 1) * (NB // LANES)):
                    conv_unit(j, cl)
        out_a_item()
        for hcol in range(W_B // NB):
            headnorm_item(hcol)
        out_item()
        return carry

    lax.fori_loop(0, nchunk, chunk, 0)


def _mixer_call(x, shift, gs, gate, w_in, caw, cbw, cbb, lng, lnb, mmat, w_out, fg):
    B, L, D = x.shape
    T = SEQ_TILE
    assert L % T == 0 and T % CHUNK == 0 and CHUNK % CONV_ROWS == 0
    nt = L // T
    ntiles = B * nt
    x2 = x.reshape(B * L, D)

    def const(shape):
        return pl.BlockSpec(shape, lambda s: (0,) * len(shape),
                            pipeline_mode=pl.Buffered(1))

    def tile_a(s):
        return jnp.minimum(s, ntiles - 1)

    def tile_b(s):
        return jnp.maximum(s - 1, 0)

    out = pl.pallas_call(
        functools.partial(_mixer_kernel, nt=nt),
        grid=(ntiles + 1,),
        in_specs=[
            pl.BlockSpec((T, D), lambda s: (tile_a(s), 0)),
            pl.BlockSpec((T, D), lambda s: (tile_b(s), 0)),
            pl.BlockSpec((1, 1, D), lambda s: (tile_a(s) // nt, 0, 0)),
            pl.BlockSpec((1, 1, D), lambda s: (tile_a(s) // nt, 0, 0)),
            pl.BlockSpec((1, 1, D), lambda s: (tile_b(s) // nt, 0, 0)),
            const((D, IN_COLS)),
            const((CONV_A, W_A)),
            const((CONV_B, W_B)),
            const((1, W_B)),
            const((1, W_B)),
            const((1, W_B)),
            const((MXU_DIM, MXU_DIM)),
            const((D, D)),
            const((1, D)),
        ],
        out_specs=pl.BlockSpec((T, D), lambda s: (tile_b(s), 0)),
        out_shape=jax.ShapeDtypeStruct((B * L, D), F32),
        scratch_shapes=[
            pltpu.VMEM((CHUNK, D), BF16),
            pltpu.VMEM((NSLAB, 2 * SLOT, LANES), F32),
            pltpu.VMEM((2, T, W_A), F32),
            pltpu.VMEM((NSLAB, 2 * SLOT, LANES), F32),
            pltpu.VMEM((2, T, W_B), F32),
            pltpu.VMEM((CHUNK, D), BF16),
            pltpu.VMEM((CHUNK, W_B), F32),
            pltpu.VMEM((CHUNK, W_B), F32),
            pltpu.VMEM((CHUNK, D), F32),
        ],
        compiler_params=pltpu.CompilerParams(
            dimension_semantics=("arbitrary",), vmem_limit_bytes=VMEM_LIMIT),
        name="mixer_block",
    )(x2, x2, shift, gs, gate, w_in, caw, cbw, cbb, lng, lnb, mmat, w_out, fg)
    return out.reshape(B, L, D)


def _permute_in_cols(w):
    def slab(col, j):
        return w[:, col + j * LANES:col + (j + 1) * LANES]
    parts = []
    for c1, c2 in ((C_BV, C_BG), (C_AIN, C_AC), (C_AB, C_AZ)):
        for j in range(NSLAB):
            parts += [slab(c1, j), slab(c2, j)]
    parts.append(w[:, C_BZ:C_BZ + W_B])
    return jnp.concatenate(parts, axis=1)


def kernel(x_prompt, x_sample, c_prompt, c_sample, norm_g, w_ada, b_ada, w_in, conv_a_w,
           conv_b_w, conv_b_b, ln_g, ln_b, w_out, final_g):
    nb_p = c_prompt.shape[0]
    c_all = jnp.concatenate([c_prompt, c_sample], axis=0)
    mod = _ada_call(c_all, w_ada[0], b_ada[0], norm_g[0])
    mod = mod[:, :, None, :]

    w_in_b = _permute_in_cols(w_in[0]).astype(BF16)
    w_out_b = w_out[0].astype(BF16)
    head = jnp.arange(MXU_DIM) // HEAD_DIM
    mmat = jnp.where(head[:, None] == head[None, :], 1.0 / HEAD_DIM, 0.0).astype(BF16)

    def run(x, lo, hi):
        return _mixer_call(x, mod[0, lo:hi], mod[1, lo:hi], mod[2, lo:hi], w_in_b,
                           conv_a_w[0], conv_b_w[0], conv_b_b[0].reshape(1, -1),
                           ln_g[0].reshape(1, -1), ln_b[0].reshape(1, -1), mmat, w_out_b,
                           final_g.reshape(1, -1))

    y_prompt = run(x_prompt, 0, nb_p)
    y_sample = run(x_sample, nb_p, c_all.shape[0])
    return (y_prompt, y_sample)
```
